```python
import jax, jax.numpy as jnp
from jax import lax
import numpy as np

D_MODEL = 2048
BATCH = 2
SEQ = 4096
DEPTH = 1

N_META = 16
POOL_WIDTH = D_MODEL
POOL_WINDOWS = (2, 4, 8, 16)
POOL_GROUPS = len(POOL_WINDOWS)
POOL_GROUP_DIM = POOL_WIDTH // POOL_GROUPS
LRU_WIDTH = D_MODEL
LRU_HEAD_DIM = 256
LRU_HEADS = LRU_WIDTH // LRU_HEAD_DIM
CONV_WIDTH = 4
LRU_C = 8.0
D_FF = 4 * D_MODEL
NORM_EPS = 1e-6
IN_SPLITS = (POOL_WIDTH,
             POOL_WIDTH + LRU_WIDTH,
             POOL_WIDTH + 2 * LRU_WIDTH,
             POOL_WIDTH + 2 * LRU_WIDTH + D_MODEL)
IN_COLS = POOL_WIDTH + 2 * LRU_WIDTH + 2 * D_MODEL

kernel_name = "hybrid_pool_rglru_gated_block"


def rmsnorm(x, g):
    xf = x.astype(jnp.float32)
    y = xf * lax.rsqrt(jnp.mean(xf * xf, axis=-1, keepdims=True) + NORM_EPS)
    return (y * g.astype(jnp.float32)).astype(x.dtype)


def causal_window_mean(v, w):
    T = v.shape[1]
    c = lax.cumsum(v, axis=1)
    c_shift = jnp.pad(c, ((0, 0), (w, 0), (0, 0)))[:, :T]
    cnt = jnp.minimum(jnp.arange(1, T + 1), w).astype(jnp.float32)
    return (c - c_shift) / cnt[None, :, None]


def pool_mixer(v, pool_w, pool_scale):
    B, T, _ = v.shape
    vf = v.astype(jnp.float32)
    diffs = []
    for g, w in enumerate(POOL_WINDOWS):
        vg = vf[..., g * POOL_GROUP_DIM:(g + 1) * POOL_GROUP_DIM]
        diffs.append(causal_window_mean(vg, w) - vg)
    d = jnp.stack(diffs, axis=2)
    y = jnp.einsum('btgc,gcd->btgd', d, pool_w.astype(jnp.float32))
    y = y.reshape(B, T, POOL_WIDTH) * pool_scale.astype(jnp.float32)
    return y.astype(v.dtype)


def causal_depthwise_conv(x, w, b):
    y = lax.conv_general_dilated(
        x, w[:, None, :].astype(x.dtype), window_strides=(1,),
        padding=((CONV_WIDTH - 1, 0),),
        dimension_numbers=('NWC', 'WIO', 'NWC'),
        feature_group_count=x.shape[-1])
    return y + b.astype(x.dtype)


def rg_lru(xc, gate_a_w, gate_a_b, gate_x_w, gate_x_b, lam):
    B, T, W = xc.shape
    xf = xc.astype(jnp.float32)
    xh = xf.reshape(B, T, LRU_HEADS, LRU_HEAD_DIM)
    r = jax.nn.sigmoid(jnp.einsum('bthi,hij->bthj', xh, gate_a_w.astype(jnp.float32))
                       + gate_a_b.astype(jnp.float32)).reshape(B, T, W)
    i = jax.nn.sigmoid(jnp.einsum('bthi,hij->bthj', xh, gate_x_w.astype(jnp.float32))
                       + gate_x_b.astype(jnp.float32)).reshape(B, T, W)
    log_a = -LRU_C * r * jax.nn.softplus(-lam.astype(jnp.float32))
    a = jnp.exp(log_a)
    mult = jnp.sqrt(-jnp.expm1(2.0 * log_a))
    bt = mult * (i * xf)

    def combine(left, right):
        a1, b1 = left
        a2, b2 = right
        return a1 * a2, a2 * b1 + b2

    _, h = lax.associative_scan(combine, (a, bt), axis=1)
    return h.astype(xc.dtype)


def setup_inputs(seed: int = 0) -> dict:
    key = jax.random.key(seed)
    ks = jax.random.split(key, 20)
    f32 = jnp.float32
    x = jax.random.normal(ks[0], (BATCH, SEQ, D_MODEL), f32)
    meta_tokens = jax.random.normal(ks[1], (N_META, D_MODEL), f32)
    norm1_g = 1.0 + 0.02 * jax.random.normal(ks[2], (DEPTH, D_MODEL), f32)
    w_in = jax.random.normal(ks[3], (DEPTH, D_MODEL, IN_COLS), f32) * D_MODEL ** -0.5
    pool_w = jax.random.normal(ks[4], (DEPTH, POOL_GROUPS, POOL_GROUP_DIM, POOL_GROUP_DIM), f32) * POOL_GROUP_DIM ** -0.5
    pool_scale = 1.0 + 0.02 * jax.random.normal(ks[5], (DEPTH, POOL_WIDTH), f32)
    conv_w = jax.random.normal(ks[6], (DEPTH, CONV_WIDTH, LRU_WIDTH), f32) * CONV_WIDTH ** -0.5
    conv_b = 0.01 * jax.random.normal(ks[7], (DEPTH, LRU_WIDTH), f32)
    gate_a_w = jax.random.normal(ks[8], (DEPTH, LRU_HEADS, LRU_HEAD_DIM, LRU_HEAD_DIM), f32) * LRU_HEAD_DIM ** -0.5
    gate_a_b = 0.01 * jax.random.normal(ks[9], (DEPTH, LRU_HEADS, LRU_HEAD_DIM), f32)
    gate_x_w = jax.random.normal(ks[10], (DEPTH, LRU_HEADS, LRU_HEAD_DIM, LRU_HEAD_DIM), f32) * LRU_HEAD_DIM ** -0.5
    gate_x_b = 0.01 * jax.random.normal(ks[11], (DEPTH, LRU_HEADS, LRU_HEAD_DIM), f32)
    u = jax.random.uniform(ks[12], (DEPTH, LRU_WIDTH), f32, minval=0.9, maxval=0.999)
    s = u ** (1.0 / LRU_C)
    lru_lambda = jnp.log(s) - jnp.log1p(-s)
    w_out = jax.random.normal(ks[13], (DEPTH, D_MODEL, D_MODEL), f32) * D_MODEL ** -0.5
    norm2_g = 1.0 + 0.02 * jax.random.normal(ks[14], (DEPTH, D_MODEL), f32)
    mlp_w1 = jax.random.normal(ks[15], (DEPTH, D_MODEL, D_FF), f32) * D_MODEL ** -0.5
    mlp_w2 = jax.random.normal(ks[16], (DEPTH, D_FF, D_MODEL), f32) * D_FF ** -0.5
    final_g = 1.0 + 0.02 * jax.random.normal(ks[17], (D_MODEL,), f32)
    return {"x": x, "meta_tokens": meta_tokens, "norm1_g": norm1_g, "w_in": w_in,
            "pool_w": pool_w, "pool_scale": pool_scale, "conv_w": conv_w, "conv_b": conv_b,
            "gate_a_w": gate_a_w, "gate_a_b": gate_a_b, "gate_x_w": gate_x_w, "gate_x_b": gate_x_b,
            "lru_lambda": lru_lambda, "w_out": w_out, "norm2_g": norm2_g,
            "mlp_w1": mlp_w1, "mlp_w2": mlp_w2, "final_g": final_g}


def reference(x, meta_tokens, norm1_g, w_in, pool_w, pool_scale, conv_w, conv_b,
              gate_a_w, gate_a_b, gate_x_w, gate_x_b, lru_lambda, w_out, norm2_g,
              mlp_w1, mlp_w2, final_g):
    B = x.shape[0]
    meta = jnp.broadcast_to(meta_tokens[None].astype(x.dtype), (B, N_META, x.shape[-1]))
    h = jnp.concatenate([meta, x], axis=1)
    for l in range(DEPTH):
        u = rmsnorm(h, norm1_g[l])
        proj = u @ w_in[l]
        v_pool, v_lru, v_gelu, g_pool, g_lru = jnp.split(proj, IN_SPLITS, axis=-1)
        pool_out = pool_mixer(v_pool, pool_w[l], pool_scale[l])
        xc = causal_depthwise_conv(v_lru, conv_w[l], conv_b[l])
        lru_out = rg_lru(xc, gate_a_w[l], gate_a_b[l], gate_x_w[l], gate_x_b[l],
                         lru_lambda[l]) * jax.nn.gelu(v_gelu)
        merged = jax.nn.sigmoid(g_pool) * pool_out + jax.nn.sigmoid(g_lru) * lru_out
        h = h + merged @ w_out[l]
        u2 = rmsnorm(h, norm2_g[l])
        h = h + jnp.square(jax.nn.relu(u2 @ mlp_w1[l])) @ mlp_w2[l]
    out = rmsnorm(h, final_g)
    return out[:, N_META:]
```

```python
import functools

import jax
import jax.numpy as jnp
from jax import lax
from jax.experimental import pallas as pl
from jax.experimental.pallas import tpu as pltpu

N_META = 16
POOL_WINDOWS = (2, 4, 8, 16)
POOL_GROUP_DIM = 512
LRU_HEAD_DIM = 256
CONV_WIDTH = 4
LRU_C = 8.0
NORM_EPS = 1e-6

SUBLANES = 8
CB = POOL_GROUP_DIM
HEADS_PER_CB = CB // LRU_HEAD_DIM
MAX_WIN = max(POOL_WINDOWS)
CONV_HIST = SUBLANES
V7X_SCOPED_VMEM_BYTES = 60000 * 1024

F32 = jnp.float32
BF16 = jnp.bfloat16


def _rmsnorm(x, g):
    return x * lax.rsqrt(jnp.mean(x * x, axis=-1, keepdims=True) + NORM_EPS) * g


def _dot(a, b):
    return jnp.dot(a, b, preferred_element_type=F32)


def _window_sums(pa_ref, pb_ref, hist, vp, n):
    m = n + MAX_WIN
    zeros = jnp.zeros((MAX_WIN, CB), F32)
    pa_ref[0:MAX_WIN, :] = zeros
    pb_ref[0:MAX_WIN, :] = zeros
    pa_ref[MAX_WIN:2 * MAX_WIN, :] = hist
    pa_ref[2 * MAX_WIN:, :] = vp
    src, dst = pa_ref, pb_ref
    s = src[MAX_WIN:, :]
    sums = []
    shift = 1
    while shift < MAX_WIN:
        s = s + src[MAX_WIN - shift:MAX_WIN - shift + m, :]
        sums.append(s)
        shift *= 2
        if shift < MAX_WIN:
            dst[MAX_WIN:, :] = s
            src, dst = dst, src
    return sums


def _causal_conv(l_ref, hist, vl, cw, cbias, n):
    l_ref[0:CONV_HIST, :] = hist
    l_ref[CONV_HIST:, :] = vl
    acc = cbias + cw[CONV_WIDTH - 1:CONV_WIDTH, :] * vl
    for k in range(CONV_WIDTH - 1):
        off = CONV_HIST - (CONV_WIDTH - 1) + k
        acc = acc + cw[k:k + 1, :] * l_ref[off:off + n, :]
    return acc


def _lru_coeffs(xc, gaw_ref, gab, gxw_ref, gxb, lam):
    xcb = xc.astype(BF16)
    r_parts, i_parts = [], []
    for h in range(HEADS_PER_CB):
        xh = xcb[:, h * LRU_HEAD_DIM:(h + 1) * LRU_HEAD_DIM]
        r_parts.append(_dot(xh, gaw_ref[h]))
        i_parts.append(_dot(xh, gxw_ref[h]))
    r = jax.nn.sigmoid(jnp.concatenate(r_parts, axis=-1) + gab)
    i = jax.nn.sigmoid(jnp.concatenate(i_parts, axis=-1) + gxb)
    nlam = -lam
    softplus = jnp.maximum(nlam, 0.0) + jnp.log1p(jnp.exp(-jnp.abs(nlam)))
    log_a = (-LRU_C) * r * softplus
    a = jnp.exp(log_a)
    mult = jnp.sqrt(1.0 - a * a)
    return a, mult * (i * xc)


def _linear_scan(a, b, h_prev, a_ref, b_ref, h_ref, n):
    rows = lax.broadcasted_iota(jnp.int32, a.shape, 0) & (SUBLANES - 1)
    k = 1
    while k < SUBLANES:
        inside = rows >= k
        a_sh = jnp.where(inside, pltpu.roll(a, k, 0), 1.0)
        b_sh = jnp.where(inside, pltpu.roll(b, k, 0), 0.0)
        b = a * b_sh + b
        a = a * a_sh
        k *= 2
    a_ref[...] = a
    b_ref[...] = b
    hp = h_prev
    for g in range(n // SUBLANES):
        sl = pl.ds(g * SUBLANES, SUBLANES)
        hg = a_ref[sl, :] * hp + b_ref[sl, :]
        h_ref[sl, :] = hg
        hp = jnp.broadcast_to(hg[SUBLANES - 1:SUBLANES, :], (SUBLANES, CB))
    return hp


def _meta_kernel(meta_ref, g1_ref, wp_ref, wl_ref, cw_ref, cbias_ref, gaw_ref, gab_ref,
                 gxw_ref, gxb_ref, lam_ref, vp_out, vl_out, h_out,
                 u_scr, l_scr, a_scr, b_scr, h_scr):
    c = pl.program_id(0)

    @pl.when(c == 0)
    def _():
        u_scr[...] = _rmsnorm(meta_ref[...], g1_ref[...]).astype(BF16)

    u = u_scr[...]
    vp = _dot(u, wp_ref[...])
    vl = _dot(u, wl_ref[...])
    vp_out[...] = vp
    vl_out[...] = vl[N_META - CONV_HIST:, :]
    xc = _causal_conv(l_scr, jnp.zeros((CONV_HIST, CB), F32), vl, cw_ref[...], cbias_ref[...], N_META)
    a, b = _lru_coeffs(xc, gaw_ref, gab_ref[...], gxw_ref, gxb_ref[...], lam_ref[...])
    h_out[...] = _linear_scan(a, b, jnp.zeros((SUBLANES, CB), F32), a_scr, b_scr, h_scr, N_META)


def _mixer_kernel(x_ref, g1_ref, w0_ref, w1_ref, w2_ref, w3_ref, w4_ref, pw_ref, ps_ref,
                  cw_ref, cbias_ref, gaw_ref, gab_ref, gxw_ref, gxb_ref, lam_ref, wo_ref,
                  vpm_ref, vlm_ref, hm_ref, out_ref,
                  u_scr, pa_scr, pb_scr, l_scr, a_scr, b_scr, h_scr,
                  pool_hist, conv_hist, h_state, *, tm):
    t = pl.program_id(1)
    c = pl.program_id(2)

    @pl.when(c == 0)
    def _():
        xt = x_ref[0]
        u_scr[...] = _rmsnorm(xt, g1_ref[...]).astype(BF16)
        out_ref[0] = xt

    @pl.when(t == 0)
    def _():
        pool_hist[c] = vpm_ref[...]
        conv_hist[c] = vlm_ref[...]
        h_state[c] = hm_ref[...]

    u = u_scr[...]

    vp = _dot(u, w0_ref[...])
    sums = _window_sums(pa_scr, pb_scr, pool_hist[c], vp, tm)
    pool_hist[c] = vp[tm - MAX_WIN:, :]
    s = sums[-1]
    inv_w = jnp.float32(1.0 / POOL_WINDOWS[-1])
    for g in range(len(POOL_WINDOWS) - 2, -1, -1):
        s = jnp.where(c == g, sums[g], s)
        inv_w = jnp.where(c == g, jnp.float32(1.0 / POOL_WINDOWS[g]), inv_w)
    d = s[MAX_WIN:, :] * inv_w - vp
    y_pool = _dot(d.astype(BF16), pw_ref[0]) * ps_ref[...]

    vl = _dot(u, w1_ref[...])
    xc = _causal_conv(l_scr, conv_hist[c], vl, cw_ref[...], cbias_ref[...], tm)
    conv_hist[c] = vl[tm - CONV_HIST:, :]
    a, b = _lru_coeffs(xc, gaw_ref, gab_ref[...], gxw_ref, gxb_ref[...], lam_ref[...])
    h_state[c] = _linear_scan(a, b, h_state[c], a_scr, b_scr, h_scr, tm)
    lru_out = h_scr[...] * jax.nn.gelu(_dot(u, w2_ref[...]))

    merged = (jax.nn.sigmoid(_dot(u, w3_ref[...])) * y_pool
              + jax.nn.sigmoid(_dot(u, w4_ref[...])) * lru_out)
    out_ref[0] += _dot(merged.astype(BF16), wo_ref[...])


def _mlp_kernel(h_ref, g2_ref, w1_ref, w2_ref, gf_ref, out_ref, u_scr):
    f = pl.program_id(1)

    @pl.when(f == 0)
    def _():
        h = h_ref[...]
        u_scr[...] = _rmsnorm(h, g2_ref[...]).astype(BF16)
        out_ref[...] = h

    a = jnp.square(jnp.maximum(_dot(u_scr[...], w1_ref[...]), 0.0))
    out_ref[...] += _dot(a.astype(BF16), w2_ref[...])

    @pl.when(f == pl.num_programs(1) - 1)
    def _():
        out_ref[...] = _rmsnorm(out_ref[...], gf_ref[...])


def _tiles(seq, d_ff):
    tm = 512 if seq % 512 == 0 else seq
    tf = 1024 if d_ff % 1024 == 0 else d_ff
    return tm, tm, tf


def kernel(x, meta_tokens, norm1_g, w_in, pool_w, pool_scale, conv_w, conv_b, gate_a_w, gate_a_b,
           gate_x_w, gate_x_b, lru_lambda, w_out, norm2_g, mlp_w1, mlp_w2, final_g):
    batch, seq, d = x.shape
    depth = w_in.shape[0]
    d_ff = mlp_w1.shape[-1]
    assert depth == 1 and meta_tokens.shape == (N_META, d)
    assert d % CB == 0 and pool_w.shape[1:] == (d // CB, CB, CB)
    assert w_in.shape == (1, d, 5 * d) and conv_w.shape == (1, CONV_WIDTH, d)
    nc = d // CB
    tm, tmb, tf = _tiles(seq, d_ff)
    assert seq % tm == 0 and tm % SUBLANES == 0 and tm >= MAX_WIN

    row = lambda v: v.reshape(1, -1).astype(F32)
    w_in_b = w_in[0].astype(BF16)
    pool_w_b = pool_w[0].astype(BF16)
    gaw_b = gate_a_w[0].astype(BF16)
    gxw_b = gate_x_w[0].astype(BF16)
    w_out_b = w_out[0].astype(BF16)
    w1_b = mlp_w1[0].astype(BF16)
    w2_b = mlp_w2[0].astype(BF16)
    g1, g2, gf = row(norm1_g[0]), row(norm2_g[0]), row(final_g)
    ps, cbias, lam = row(pool_scale[0]), row(conv_b[0]), row(lru_lambda[0])
    gab, gxb = row(gate_a_b[0]), row(gate_x_b[0])
    cw = conv_w[0]

    chan = lambda c: (0, c)
    meta_specs = [
        pl.BlockSpec((N_META, d), lambda c: (0, 0)),
        pl.BlockSpec((1, d), lambda c: (0, 0)),
        pl.BlockSpec((d, CB), lambda c: (0, c)),
        pl.BlockSpec((d, CB), lambda c: (0, nc + c)),
        pl.BlockSpec((CONV_WIDTH, CB), chan),
        pl.BlockSpec((1, CB), chan),
        pl.BlockSpec((HEADS_PER_CB, LRU_HEAD_DIM, LRU_HEAD_DIM), lambda c: (c, 0, 0)),
        pl.BlockSpec((1, CB), chan),
        pl.BlockSpec((HEADS_PER_CB, LRU_HEAD_DIM, LRU_HEAD_DIM), lambda c: (c, 0, 0)),
        pl.BlockSpec((1, CB), chan),
        pl.BlockSpec((1, CB), chan),
    ]
    vp_meta, vl_meta, h_meta = pl.pallas_call(
        _meta_kernel,
        grid=(nc,),
        in_specs=meta_specs,
        out_specs=[pl.BlockSpec((N_META, CB), chan), pl.BlockSpec((CONV_HIST, CB), chan),
                   pl.BlockSpec((SUBLANES, CB), chan)],
        out_shape=[jax.ShapeDtypeStruct((N_META, d), F32), jax.ShapeDtypeStruct((CONV_HIST, d), F32),
                   jax.ShapeDtypeStruct((SUBLANES, d), F32)],
        scratch_shapes=[pltpu.VMEM((N_META, d), BF16), pltpu.VMEM((N_META + CONV_HIST, CB), F32),
                        pltpu.VMEM((N_META, CB), F32), pltpu.VMEM((N_META, CB), F32),
                        pltpu.VMEM((N_META, CB), F32)],
        compiler_params=pltpu.CompilerParams(dimension_semantics=("arbitrary",)),
        name="meta_prologue",
    )(meta_tokens, g1, w_in_b, w_in_b, cw, cbias, gaw_b, gab, gxw_b, gxb, lam)

    chan3 = lambda b, t, c: (0, c)
    w_in_spec = lambda j: pl.BlockSpec((d, CB), lambda b, t, c, j=j: (0, j * nc + c))
    head_spec = pl.BlockSpec((HEADS_PER_CB, LRU_HEAD_DIM, LRU_HEAD_DIM), lambda b, t, c: (c, 0, 0))
    x_spec = pl.BlockSpec((1, tm, d), lambda b, t, c: (b, t, 0))
    mixer_specs = [
        x_spec,
        pl.BlockSpec((1, d), lambda b, t, c: (0, 0)),
        w_in_spec(0), w_in_spec(1), w_in_spec(2), w_in_spec(3), w_in_spec(4),
        pl.BlockSpec((1, CB, CB), lambda b, t, c: (c, 0, 0)),
        pl.BlockSpec((1, CB), chan3),
        pl.BlockSpec((CONV_WIDTH, CB), chan3),
        pl.BlockSpec((1, CB), chan3),
        head_spec, pl.BlockSpec((1, CB), chan3),
        head_spec, pl.BlockSpec((1, CB), chan3),
        pl.BlockSpec((1, CB), chan3),
        pl.BlockSpec((CB, d), lambda b, t, c: (c, 0)),
        pl.BlockSpec((N_META, CB), chan3),
        pl.BlockSpec((CONV_HIST, CB), chan3),
        pl.BlockSpec((SUBLANES, CB), chan3),
    ]
    h1 = pl.pallas_call(
        functools.partial(_mixer_kernel, tm=tm),
        grid=(batch, seq // tm, nc),
        in_specs=mixer_specs,
        out_specs=x_spec,
        out_shape=jax.ShapeDtypeStruct((batch, seq, d), F32),
        scratch_shapes=[
            pltpu.VMEM((tm, d), BF16),
            pltpu.VMEM((tm + 2 * MAX_WIN, CB), F32), pltpu.VMEM((tm + 2 * MAX_WIN, CB), F32),
            pltpu.VMEM((tm + CONV_HIST, CB), F32),
            pltpu.VMEM((tm, CB), F32), pltpu.VMEM((tm, CB), F32), pltpu.VMEM((tm, CB), F32),
            pltpu.VMEM((nc, MAX_WIN, CB), F32), pltpu.VMEM((nc, CONV_HIST, CB), F32),
            pltpu.VMEM((nc, SUBLANES, CB), F32),
        ],
        compiler_params=pltpu.CompilerParams(
            dimension_semantics=("arbitrary", "arbitrary", "arbitrary"),
            vmem_limit_bytes=V7X_SCOPED_VMEM_BYTES),
        name="mixer",
    )(x, g1, w_in_b, w_in_b, w_in_b, w_in_b, w_in_b, pool_w_b, ps, cw, cbias,
      gaw_b, gab, gxw_b, gxb, lam, w_out_b, vp_meta, vl_meta, h_meta)

    rows = batch * seq
    out = pl.pallas_call(
        _mlp_kernel,
        grid=(rows // tmb, d_ff // tf),
        in_specs=[
            pl.BlockSpec((tmb, d), lambda m, f: (m, 0)),
            pl.BlockSpec((1, d), lambda m, f: (0, 0)),
            pl.BlockSpec((d, tf), lambda m, f: (0, f)),
            pl.BlockSpec((tf, d), lambda m, f: (f, 0)),
            pl.BlockSpec((1, d), lambda m, f: (0, 0)),
        ],
        out_specs=pl.BlockSpec((tmb, d), lambda m, f: (m, 0)),
        out_shape=jax.ShapeDtypeStruct((rows, d), F32),
        scratch_shapes=[pltpu.VMEM((tmb, d), BF16)],
        compiler_params=pltpu.CompilerParams(
            dimension_semantics=("arbitrary", "arbitrary"),
            vmem_limit_bytes=V7X_SCOPED_VMEM_BYTES),
        name="mlp",
    )(h1.reshape(rows, d), g2, w1_b, w2_b, gf)
    return out.reshape(batch, seq, d)
```

```python
import functools

import jax
import jax.numpy as jnp
from jax import lax
from jax.experimental import pallas as pl
from jax.experimental.pallas import tpu as pltpu

N_META = 16
POOL_WINDOWS = (2, 4, 8, 16)
POOL_GROUP_DIM = 512
LRU_HEAD_DIM = 256
CONV_WIDTH = 4
LRU_C = 8.0
NORM_EPS = 1e-6

SUBLANES = 8
CB = POOL_GROUP_DIM
HEADS_PER_CB = CB // LRU_HEAD_DIM
MAX_WIN = max(POOL_WINDOWS)
CONV_HIST = SUBLANES
V7X_SCOPED_VMEM_BYTES = 60000 * 1024

F32 = jnp.float32
BF16 = jnp.bfloat16


def _rmsnorm(x, g):
    return x * lax.rsqrt(jnp.mean(x * x, axis=-1, keepdims=True) + NORM_EPS) * g


_dot = functools.partial(jnp.dot, preferred_element_type=F32)


def _shift_rows(e, k):
    return pltpu.roll(e, k, 0)


def _window_sums(hist, vp):
    s = jnp.concatenate([hist, vp], axis=0)
    sums = []
    shift = 1
    while shift < MAX_WIN:
        s = s + _shift_rows(s, shift)
        sums.append(s)
        shift *= 2
    return sums


def _causal_conv(hist, vl, cw, cbias):
    e = jnp.concatenate([hist, vl], axis=0)
    acc = cbias + cw[CONV_WIDTH - 1:CONV_WIDTH, :] * e
    for k in range(CONV_WIDTH - 1):
        acc = acc + cw[k:k + 1, :] * _shift_rows(e, CONV_WIDTH - 1 - k)
    return acc[CONV_HIST:, :]


def _lru_coeffs(xc, gaw_ref, gab, gxw_ref, gxb, lam):
    xcb = xc.astype(BF16)
    r_parts, i_parts = [], []
    for h in range(HEADS_PER_CB):
        xh = xcb[:, h * LRU_HEAD_DIM:(h + 1) * LRU_HEAD_DIM]
        r_parts.append(_dot(xh, gaw_ref[h]))
        i_parts.append(_dot(xh, gxw_ref[h]))
    r = jax.nn.sigmoid(jnp.concatenate(r_parts, axis=-1) + gab)
    i = jax.nn.sigmoid(jnp.concatenate(i_parts, axis=-1) + gxb)
    nlam = -lam
    softplus = jnp.maximum(nlam, 0.0) + jnp.log1p(jnp.exp(-jnp.abs(nlam)))
    log_a = (-LRU_C) * r * softplus
    a = jnp.exp(log_a)
    mult = jnp.sqrt(1.0 - a * a)
    return a, mult * (i * xc)


def _linear_scan(a, b, h_prev, a_ref, b_ref, h_ref, n):
    a = a.reshape(n // SUBLANES, SUBLANES, CB)
    b = b.reshape(n // SUBLANES, SUBLANES, CB)
    rows = lax.broadcasted_iota(jnp.int32, a.shape, 1)
    k = 1
    while k < SUBLANES:
        inside = rows >= k
        a_sh = jnp.where(inside, pltpu.roll(a, k, 1), 1.0)
        b_sh = jnp.where(inside, pltpu.roll(b, k, 1), 0.0)
        b = a * b_sh + b
        a = a * a_sh
        k *= 2
    a = a.reshape(n, CB)
    b = b.reshape(n, CB)
    a_ref[...] = a
    b_ref[...] = b
    hp = h_prev
    for g in range(n // SUBLANES):
        sl = pl.ds(g * SUBLANES, SUBLANES)
        hg = a_ref[sl, :] * hp + b_ref[sl, :]
        h_ref[sl, :] = hg
        hp = jnp.broadcast_to(hg[SUBLANES - 1:SUBLANES, :], (SUBLANES, CB))
    return hp


def _meta_kernel(meta_ref, g1_ref, wp_ref, wl_ref, cw_ref, cbias_ref, gaw_ref, gab_ref,
                 gxw_ref, gxb_ref, lam_ref, vp_out, vl_out, h_out,
                 u_scr, a_scr, b_scr, h_scr):
    c = pl.program_id(0)

    @pl.when(c == 0)
    def _():
        u_scr[...] = _rmsnorm(meta_ref[...], g1_ref[...]).astype(BF16)

    u = u_scr[...]
    vp = _dot(u, wp_ref[...])
    vl = _dot(u, wl_ref[...])
    vp_out[...] = vp
    vl_out[...] = vl[N_META - CONV_HIST:, :]
    xc = _causal_conv(jnp.zeros((CONV_HIST, CB), F32), vl, cw_ref[...], cbias_ref[...])
    a, b = _lru_coeffs(xc, gaw_ref, gab_ref[...], gxw_ref, gxb_ref[...], lam_ref[...])
    h_out[...] = _linear_scan(a, b, jnp.zeros((SUBLANES, CB), F32), a_scr, b_scr, h_scr, N_META)


def _mixer_kernel(x_ref, g1_ref, w0_ref, w1_ref, w2_ref, w3_ref, w4_ref, pw_ref, ps_ref,
                  cw_ref, cbias_ref, gaw_ref, gab_ref, gxw_ref, gxb_ref, lam_ref, wo_ref,
                  vpm_ref, vlm_ref, hm_ref, out_ref,
                  u_scr, a_scr, b_scr, h_scr,
                  pool_hist, conv_hist, h_state, *, tm):
    t = pl.program_id(1)
    c = pl.program_id(2)

    @pl.when(c == 0)
    def _():
        xt = x_ref[0]
        u_scr[...] = _rmsnorm(xt, g1_ref[...]).astype(BF16)
        out_ref[0] = xt

    @pl.when(t == 0)
    def _():
        pool_hist[c] = vpm_ref[...]
        conv_hist[c] = vlm_ref[...]
        h_state[c] = hm_ref[...]

    u = u_scr[...]
    vl = _dot(u, w1_ref[...])
    vp = _dot(u, w0_ref[...])

    xc = _causal_conv(conv_hist[c], vl, cw_ref[...], cbias_ref[...])
    conv_hist[c] = vl[tm - CONV_HIST:, :]
    a, b = _lru_coeffs(xc, gaw_ref, gab_ref[...], gxw_ref, gxb_ref[...], lam_ref[...])
    gelu_gate = jax.nn.gelu(_dot(u, w2_ref[...]))

    sums = _window_sums(pool_hist[c], vp)
    pool_hist[c] = vp[tm - MAX_WIN:, :]
    s = sums[-1]
    inv_w = jnp.float32(1.0 / POOL_WINDOWS[-1])
    for g in range(len(POOL_WINDOWS) - 2, -1, -1):
        s = jnp.where(c == g, sums[g], s)
        inv_w = jnp.where(c == g, jnp.float32(1.0 / POOL_WINDOWS[g]), inv_w)
    d = s[MAX_WIN:, :] * inv_w - vp
    pool_gate = jax.nn.sigmoid(_dot(u, w3_ref[...]))
    pool_term = pool_gate * (_dot(d.astype(BF16), pw_ref[0]) * ps_ref[...])
    lru_gate = jax.nn.sigmoid(_dot(u, w4_ref[...])) * gelu_gate

    h_state[c] = _linear_scan(a, b, h_state[c], a_scr, b_scr, h_scr, tm)
    merged = pool_term + lru_gate * h_scr[...]
    out_ref[0] += _dot(merged.astype(BF16), wo_ref[...])


def _mlp_kernel(h_ref, g2_ref, w1_ref, w2_ref, gf_ref, out_ref, u_scr):
    f = pl.program_id(1)

    @pl.when(f == 0)
    def _():
        h = h_ref[...]
        u_scr[...] = _rmsnorm(h, g2_ref[...]).astype(BF16)
        out_ref[...] = h

    a = jnp.square(jnp.maximum(_dot(u_scr[...], w1_ref[...]), 0.0))
    out_ref[...] += _dot(a.astype(BF16), w2_ref[...])

    @pl.when(f == pl.num_programs(1) - 1)
    def _():
        out_ref[...] = _rmsnorm(out_ref[...], gf_ref[...])


def _tiles(seq, d_ff):
    tm = 512 if seq % 512 == 0 else seq
    tf = 1024 if d_ff % 1024 == 0 else d_ff
    return tm, tm, tf


def kernel(x, meta_tokens, norm1_g, w_in, pool_w, pool_scale, conv_w, conv_b, gate_a_w, gate_a_b,
           gate_x_w, gate_x_b, lru_lambda, w_out, norm2_g, mlp_w1, mlp_w2, final_g):
    batch, seq, d = x.shape
    depth = w_in.shape[0]
    d_ff = mlp_w1.shape[-1]
    assert depth == 1 and meta_tokens.shape == (N_META, d)
    assert d % CB == 0 and pool_w.shape[1:] == (d // CB, CB, CB)
    assert w_in.shape == (1, d, 5 * d) and conv_w.shape == (1, CONV_WIDTH, d)
    nc = d // CB
    tm, tmb, tf = _tiles(seq, d_ff)
    assert seq % tm == 0 and tm % SUBLANES == 0 and tm >= MAX_WIN

    row = lambda v: v.reshape(1, -1).astype(F32)
    w_in_b = w_in[0].astype(BF16)
    pool_w_b = pool_w[0].astype(BF16)
    gaw_b = gate_a_w[0].astype(BF16)
    gxw_b = gate_x_w[0].astype(BF16)
    w_out_b = w_out[0].astype(BF16)
    w1_b = mlp_w1[0].astype(BF16)
    w2_b = mlp_w2[0].astype(BF16)
    g1, g2, gf = row(norm1_g[0]), row(norm2_g[0]), row(final_g)
    ps, cbias, lam = row(pool_scale[0]), row(conv_b[0]), row(lru_lambda[0])
    gab, gxb = row(gate_a_b[0]), row(gate_x_b[0])
    cw = conv_w[0]

    chan = lambda c: (0, c)
    meta_specs = [
        pl.BlockSpec((N_META, d), lambda c: (0, 0)),
        pl.BlockSpec((1, d), lambda c: (0, 0)),
        pl.BlockSpec((d, CB), lambda c: (0, c)),
        pl.BlockSpec((d, CB), lambda c: (0, nc + c)),
        pl.BlockSpec((CONV_WIDTH, CB), chan),
        pl.BlockSpec((1, CB), chan),
        pl.BlockSpec((HEADS_PER_CB, LRU_HEAD_DIM, LRU_HEAD_DIM), lambda c: (c, 0, 0)),
        pl.BlockSpec((1, CB), chan),
        pl.BlockSpec((HEADS_PER_CB, LRU_HEAD_DIM, LRU_HEAD_DIM), lambda c: (c, 0, 0)),
        pl.BlockSpec((1, CB), chan),
        pl.BlockSpec((1, CB), chan),
    ]
    vp_meta, vl_meta, h_meta = pl.pallas_call(
        _meta_kernel,
        grid=(nc,),
        in_specs=meta_specs,
        out_specs=[pl.BlockSpec((N_META, CB), chan), pl.BlockSpec((CONV_HIST, CB), chan),
                   pl.BlockSpec((SUBLANES, CB), chan)],
        out_shape=[jax.ShapeDtypeStruct((N_META, d), F32), jax.ShapeDtypeStruct((CONV_HIST, d), F32),
                   jax.ShapeDtypeStruct((SUBLANES, d), F32)],
        scratch_shapes=[pltpu.VMEM((N_META, d), BF16),
                        pltpu.VMEM((N_META, CB), F32), pltpu.VMEM((N_META, CB), F32),
                        pltpu.VMEM((N_META, CB), F32)],
        compiler_params=pltpu.CompilerParams(dimension_semantics=("arbitrary",)),
        name="meta_prologue",
    )(meta_tokens, g1, w_in_b, w_in_b, cw, cbias, gaw_b, gab, gxw_b, gxb, lam)

    chan3 = lambda b, t, c: (0, c)
    w_in_spec = lambda j: pl.BlockSpec((d, CB), lambda b, t, c, j=j: (0, j * nc + c))
    head_spec = pl.BlockSpec((HEADS_PER_CB, LRU_HEAD_DIM, LRU_HEAD_DIM), lambda b, t, c: (c, 0, 0))
    x_spec = pl.BlockSpec((1, tm, d), lambda b, t, c: (b, t, 0))
    mixer_specs = [
        x_spec,
        pl.BlockSpec((1, d), lambda b, t, c: (0, 0)),
        w_in_spec(0), w_in_spec(1), w_in_spec(2), w_in_spec(3), w_in_spec(4),
        pl.BlockSpec((1, CB, CB), lambda b, t, c: (c, 0, 0)),
        pl.BlockSpec((1, CB), chan3),
        pl.BlockSpec((CONV_WIDTH, CB), chan3),
        pl.BlockSpec((1, CB), chan3),
        head_spec, pl.BlockSpec((1, CB), chan3),
        head_spec, pl.BlockSpec((1, CB), chan3),
        pl.BlockSpec((1, CB), chan3),
        pl.BlockSpec((CB, d), lambda b, t, c: (c, 0)),
        pl.BlockSpec((N_META, CB), chan3),
        pl.BlockSpec((CONV_HIST, CB), chan3),
        pl.BlockSpec((SUBLANES, CB), chan3),
    ]
    h1 = pl.pallas_call(
        functools.partial(_mixer_kernel, tm=tm),
        grid=(batch, seq // tm, nc),
        in_specs=mixer_specs,
        out_specs=x_spec,
        out_shape=jax.ShapeDtypeStruct((batch, seq, d), F32),
        scratch_shapes=[
            pltpu.VMEM((tm, d), BF16),
            pltpu.VMEM((tm, CB), F32), pltpu.VMEM((tm, CB), F32), pltpu.VMEM((tm, CB), F32),
            pltpu.VMEM((nc, MAX_WIN, CB), F32), pltpu.VMEM((nc, CONV_HIST, CB), F32),
            pltpu.VMEM((nc, SUBLANES, CB), F32),
        ],
        compiler_params=pltpu.CompilerParams(
            dimension_semantics=("arbitrary", "arbitrary", "arbitrary"),
            vmem_limit_bytes=V7X_SCOPED_VMEM_BYTES),
        name="mixer",
    )(x, g1, w_in_b, w_in_b, w_in_b, w_in_b, w_in_b, pool_w_b, ps, cw, cbias,
      gaw_b, gab, gxw_b, gxb, lam, w_out_b, vp_meta, vl_meta, h_meta)

    rows = batch * seq
    out = pl.pallas_call(
        _mlp_kernel,
        grid=(rows // tmb, d_ff // tf),
        in_specs=[
            pl.BlockSpec((tmb, d), lambda m, f: (m, 0)),
            pl.BlockSpec((1, d), lambda m, f: (0, 0)),
            pl.BlockSpec((d, tf), lambda m, f: (0, f)),
            pl.BlockSpec((tf, d), lambda m, f: (f, 0)),
            pl.BlockSpec((1, d), lambda m, f: (0, 0)),
        ],
        out_specs=pl.BlockSpec((tmb, d), lambda m, f: (m, 0)),
        out_shape=jax.ShapeDtypeStruct((rows, d), F32),
        scratch_shapes=[pltpu.VMEM((tmb, d), BF16)],
        compiler_params=pltpu.CompilerParams(
            dimension_semantics=("arbitrary", "arbitrary"),
            vmem_limit_bytes=V7X_SCOPED_VMEM_BYTES),
        name="mlp",
    )(h1.reshape(rows, d), g2, w1_b, w2_b, gf)
    return out.reshape(batch, seq, d)
```

```python
import functools

import jax
import jax.numpy as jnp
from jax import lax
from jax.experimental import pallas as pl
from jax.experimental.pallas import tpu as pltpu

N_META = 16
POOL_WINDOWS = (2, 4, 8, 16)
POOL_GROUP_DIM = 512
LRU_HEAD_DIM = 256
CONV_WIDTH = 4
LRU_C = 8.0
NORM_EPS = 1e-6

SUBLANES = 8
CB = POOL_GROUP_DIM
HEADS_PER_CB = CB // LRU_HEAD_DIM
MAX_WIN = max(POOL_WINDOWS)
CONV_HIST = SUBLANES
V7X_SCOPED_VMEM_BYTES = 60000 * 1024

F32 = jnp.float32
BF16 = jnp.bfloat16

_dot = functools.partial(jnp.dot, preferred_element_type=F32)


def _rmsnorm(x, g):
    return x * lax.rsqrt(jnp.mean(x * x, axis=-1, keepdims=True) + NORM_EPS) * g


def _sigmoid(x):
    return 0.5 * jnp.tanh(0.5 * x) + 0.5


def _sqrt_nonneg(y):
    return jnp.where(y > 0.0, y * lax.rsqrt(y), 0.0)


def _shift_rows(e, k):
    n, w = e.shape
    if k % SUBLANES == 0:
        return jnp.concatenate([e[n - k:], e[:n - k]], axis=0)
    assert k < SUBLANES
    r = pltpu.roll(e.reshape(n // SUBLANES, SUBLANES, w), k, 1)
    prev = jnp.concatenate([r[-1:], r[:-1]], axis=0)
    rows = lax.broadcasted_iota(jnp.int32, r.shape, 1)
    return jnp.where(rows < k, prev, r).reshape(n, w)


def _window_sums(hist, vp):
    s = jnp.concatenate([hist, vp], axis=0)
    sums = []
    shift = 1
    while shift < MAX_WIN:
        s = s + _shift_rows(s, shift)
        sums.append(s)
        shift *= 2
    return sums


def _causal_conv(hist, vl, cw, cbias):
    e = jnp.concatenate([hist, vl], axis=0)
    acc = cbias + cw[CONV_WIDTH - 1:CONV_WIDTH, :] * e
    for k in range(CONV_WIDTH - 1):
        acc = acc + cw[k:k + 1, :] * _shift_rows(e, CONV_WIDTH - 1 - k)
    return acc[CONV_HIST:, :]


def _lru_coeffs(xc, gaw_ref, gab, gxw_ref, gxb, lam):
    xcb = xc.astype(BF16)
    r_parts, i_parts = [], []
    for h in range(HEADS_PER_CB):
        xh = xcb[:, h * LRU_HEAD_DIM:(h + 1) * LRU_HEAD_DIM]
        r_parts.append(_dot(xh, gaw_ref[h]))
        i_parts.append(_dot(xh, gxw_ref[h]))
    r = _sigmoid(jnp.concatenate(r_parts, axis=-1) + gab)
    i = _sigmoid(jnp.concatenate(i_parts, axis=-1) + gxb)
    nlam = -lam
    softplus = jnp.maximum(nlam, 0.0) + jnp.log1p(jnp.exp(-jnp.abs(nlam)))
    log_a = (-LRU_C) * r * softplus
    a = jnp.exp(log_a)
    mult = _sqrt_nonneg(1.0 - a * a)
    return a, mult * (i * xc)


def _linear_scan(a, b, h_prev, a_ref, b_ref, h_ref, n):
    a = a.reshape(n // SUBLANES, SUBLANES, CB)
    b = b.reshape(n // SUBLANES, SUBLANES, CB)
    rows = lax.broadcasted_iota(jnp.int32, a.shape, 1)
    k = 1
    while k < SUBLANES:
        inside = rows >= k
        a_sh = jnp.where(inside, pltpu.roll(a, k, 1), 1.0)
        b_sh = jnp.where(inside, pltpu.roll(b, k, 1), 0.0)
        b = a * b_sh + b
        a = a * a_sh
        k *= 2
    a = a.reshape(n, CB)
    b = b.reshape(n, CB)
    a_ref[...] = a
    b_ref[...] = b
    hp = h_prev
    for g in range(n // SUBLANES):
        sl = pl.ds(g * SUBLANES, SUBLANES)
        hg = a_ref[sl, :] * hp + b_ref[sl, :]
        h_ref[sl, :] = hg
        hp = jnp.broadcast_to(hg[SUBLANES - 1:SUBLANES, :], (SUBLANES, CB))
    return hp


def _meta_kernel(meta_ref, g1_ref, wp_ref, wl_ref, cw_ref, cbias_ref, gaw_ref, gab_ref,
                 gxw_ref, gxb_ref, lam_ref, vp_out, vl_out, h_out,
                 u_scr, a_scr, b_scr, h_scr):
    c = pl.program_id(0)

    @pl.when(c == 0)
    def _():
        u_scr[...] = _rmsnorm(meta_ref[...], g1_ref[...]).astype(BF16)

    u = u_scr[...]
    vp = _dot(u, wp_ref[...])
    vl = _dot(u, wl_ref[...])
    vp_out[...] = vp
    vl_out[...] = vl[N_META - CONV_HIST:, :]
    xc = _causal_conv(jnp.zeros((CONV_HIST, CB), F32), vl, cw_ref[...], cbias_ref[...])
    a, b = _lru_coeffs(xc, gaw_ref, gab_ref[...], gxw_ref, gxb_ref[...], lam_ref[...])
    h_out[...] = _linear_scan(a, b, jnp.zeros((SUBLANES, CB), F32), a_scr, b_scr, h_scr, N_META)


def _mixer_kernel(x_ref, g1_ref, w0_ref, w1_ref, w2_ref, w3_ref, w4_ref, pw_ref, ps_ref,
                  cw_ref, cbias_ref, gaw_ref, gab_ref, gxw_ref, gxb_ref, lam_ref, wo_ref,
                  vpm_ref, vlm_ref, hm_ref, mw1_ref, mw2_ref, out_ref, mw1b_ref, mw2b_ref,
                  u_scr, a_scr, b_scr, h_scr,
                  pool_hist, conv_hist, h_state, *, tm):
    t = pl.program_id(1)
    c = pl.program_id(2)

    mw1b_ref[...] = mw1_ref[...].astype(BF16)
    mw2b_ref[...] = mw2_ref[...].astype(BF16)

    @pl.when(c == 0)
    def _():
        xt = x_ref[0]
        u_scr[...] = _rmsnorm(xt, g1_ref[...]).astype(BF16)
        out_ref[0] = xt

    @pl.when(t == 0)
    def _():
        pool_hist[c] = vpm_ref[...]
        conv_hist[c] = vlm_ref[...]
        h_state[c] = hm_ref[...]

    u = u_scr[...]
    vl = _dot(u, w1_ref[...])
    vp = _dot(u, w0_ref[...])

    xc = _causal_conv(conv_hist[c], vl, cw_ref[...], cbias_ref[...])
    conv_hist[c] = vl[tm - CONV_HIST:, :]
    a, b = _lru_coeffs(xc, gaw_ref, gab_ref[...], gxw_ref, gxb_ref[...], lam_ref[...])
    gelu_gate = jax.nn.gelu(_dot(u, w2_ref[...]))

    sums = _window_sums(pool_hist[c], vp)
    pool_hist[c] = vp[tm - MAX_WIN:, :]
    s = sums[-1]
    inv_w = jnp.float32(1.0 / POOL_WINDOWS[-1])
    for g in range(len(POOL_WINDOWS) - 2, -1, -1):
        s = jnp.where(c == g, sums[g], s)
        inv_w = jnp.where(c == g, jnp.float32(1.0 / POOL_WINDOWS[g]), inv_w)
    d = s[MAX_WIN:, :] * inv_w - vp
    pool_gate = _sigmoid(_dot(u, w3_ref[...]))
    pool_term = pool_gate * (_dot(d.astype(BF16), pw_ref[0]) * ps_ref[...])
    lru_gate = _sigmoid(_dot(u, w4_ref[...])) * gelu_gate

    h_state[c] = _linear_scan(a, b, h_state[c], a_scr, b_scr, h_scr, tm)
    merged = pool_term + lru_gate * h_scr[...]
    out_ref[0] += _dot(merged.astype(BF16), wo_ref[...])


def _mlp_kernel(h_ref, g2_ref, w1_ref, w2_ref, gf_ref, out_ref, u_scr):
    f = pl.program_id(1)

    @pl.when(f == 0)
    def _():
        h = h_ref[...]
        u_scr[...] = _rmsnorm(h, g2_ref[...]).astype(BF16)
        out_ref[...] = h

    a = jnp.square(jnp.maximum(_dot(u_scr[...], w1_ref[...]), 0.0))
    out_ref[...] += _dot(a.astype(BF16), w2_ref[...])

    @pl.when(f == pl.num_programs(1) - 1)
    def _():
        out_ref[...] = _rmsnorm(out_ref[...], gf_ref[...])


def _tiles(seq, d_ff):
    tm = 512 if seq % 512 == 0 else seq
    tf = 1024 if d_ff % 1024 == 0 else d_ff
    return tm, tm, tf


def kernel(x, meta_tokens, norm1_g, w_in, pool_w, pool_scale, conv_w, conv_b, gate_a_w, gate_a_b,
           gate_x_w, gate_x_b, lru_lambda, w_out, norm2_g, mlp_w1, mlp_w2, final_g):
    batch, seq, d = x.shape
    depth = w_in.shape[0]
    d_ff = mlp_w1.shape[-1]
    assert depth == 1 and meta_tokens.shape == (N_META, d)
    assert d % CB == 0 and pool_w.shape[1:] == (d // CB, CB, CB)
    assert w_in.shape == (1, d, 5 * d) and conv_w.shape == (1, CONV_WIDTH, d)
    nc = d // CB
    tm, tmb, tf = _tiles(seq, d_ff)
    assert seq % tm == 0 and tm % SUBLANES == 0 and tm >= MAX_WIN

    row = lambda v: v.reshape(1, -1).astype(F32)
    w_in_b = w_in[0].astype(BF16)
    pool_w_b = pool_w[0].astype(BF16)
    gaw_b = gate_a_w[0].astype(BF16)
    gxw_b = gate_x_w[0].astype(BF16)
    w_out_b = w_out[0].astype(BF16)
    g1, g2, gf = row(norm1_g[0]), row(norm2_g[0]), row(final_g)
    ps, cbias, lam = row(pool_scale[0]), row(conv_b[0]), row(lru_lambda[0])
    gab, gxb = row(gate_a_b[0]), row(gate_x_b[0])
    cw = conv_w[0]

    chan = lambda c: (0, c)
    meta_specs = [
        pl.BlockSpec((N_META, d), lambda c: (0, 0)),
        pl.BlockSpec((1, d), lambda c: (0, 0)),
        pl.BlockSpec((d, CB), lambda c: (0, c)),
        pl.BlockSpec((d, CB), lambda c: (0, nc + c)),
        pl.BlockSpec((CONV_WIDTH, CB), chan),
        pl.BlockSpec((1, CB), chan),
        pl.BlockSpec((HEADS_PER_CB, LRU_HEAD_DIM, LRU_HEAD_DIM), lambda c: (c, 0, 0)),
        pl.BlockSpec((1, CB), chan),
        pl.BlockSpec((HEADS_PER_CB, LRU_HEAD_DIM, LRU_HEAD_DIM), lambda c: (c, 0, 0)),
        pl.BlockSpec((1, CB), chan),
        pl.BlockSpec((1, CB), chan),
    ]
    vp_meta, vl_meta, h_meta = pl.pallas_call(
        _meta_kernel,
        grid=(nc,),
        in_specs=meta_specs,
        out_specs=[pl.BlockSpec((N_META, CB), chan), pl.BlockSpec((CONV_HIST, CB), chan),
                   pl.BlockSpec((SUBLANES, CB), chan)],
        out_shape=[jax.ShapeDtypeStruct((N_META, d), F32), jax.ShapeDtypeStruct((CONV_HIST, d), F32),
                   jax.ShapeDtypeStruct((SUBLANES, d), F32)],
        scratch_shapes=[pltpu.VMEM((N_META, d), BF16),
                        pltpu.VMEM((N_META, CB), F32), pltpu.VMEM((N_META, CB), F32),
                        pltpu.VMEM((N_META, CB), F32)],
        compiler_params=pltpu.CompilerParams(dimension_semantics=("arbitrary",)),
        name="meta_prologue",
    )(meta_tokens, g1, w_in_b, w_in_b, cw, cbias, gaw_b, gab, gxw_b, gxb, lam)

    nt = seq // tm
    nsteps = batch * nt * nc
    bf16_rows = 2 * SUBLANES
    assert d % (nsteps * bf16_rows) == 0 and d_ff % (nsteps * bf16_rows) == 0
    step_slab = lambda b, t, c: ((b * nt + t) * nc + c, 0)
    chan3 = lambda b, t, c: (0, c)
    w_in_spec = lambda j: pl.BlockSpec((d, CB), lambda b, t, c, j=j: (0, j * nc + c))
    head_spec = pl.BlockSpec((HEADS_PER_CB, LRU_HEAD_DIM, LRU_HEAD_DIM), lambda b, t, c: (c, 0, 0))
    x_spec = pl.BlockSpec((1, tm, d), lambda b, t, c: (b, t, 0))
    mixer_specs = [
        x_spec,
        pl.BlockSpec((1, d), lambda b, t, c: (0, 0)),
        w_in_spec(0), w_in_spec(1), w_in_spec(2), w_in_spec(3), w_in_spec(4),
        pl.BlockSpec((1, CB, CB), lambda b, t, c: (c, 0, 0)),
        pl.BlockSpec((1, CB), chan3),
        pl.BlockSpec((CONV_WIDTH, CB), chan3),
        pl.BlockSpec((1, CB), chan3),
        head_spec, pl.BlockSpec((1, CB), chan3),
        head_spec, pl.BlockSpec((1, CB), chan3),
        pl.BlockSpec((1, CB), chan3),
        pl.BlockSpec((CB, d), lambda b, t, c: (c, 0)),
        pl.BlockSpec((N_META, CB), chan3),
        pl.BlockSpec((CONV_HIST, CB), chan3),
        pl.BlockSpec((SUBLANES, CB), chan3),
        pl.BlockSpec((d // nsteps, d_ff), step_slab),
        pl.BlockSpec((d_ff // nsteps, d), step_slab),
    ]
    h1, w1_b, w2_b = pl.pallas_call(
        functools.partial(_mixer_kernel, tm=tm),
        grid=(batch, nt, nc),
        in_specs=mixer_specs,
        out_specs=[x_spec, pl.BlockSpec((d // nsteps, d_ff), step_slab),
                   pl.BlockSpec((d_ff // nsteps, d), step_slab)],
        out_shape=[jax.ShapeDtypeStruct((batch, seq, d), F32),
                   jax.ShapeDtypeStruct((d, d_ff), BF16), jax.ShapeDtypeStruct((d_ff, d), BF16)],
        scratch_shapes=[
            pltpu.VMEM((tm, d), BF16),
            pltpu.VMEM((tm, CB), F32), pltpu.VMEM((tm, CB), F32), pltpu.VMEM((tm, CB), F32),
            pltpu.VMEM((nc, MAX_WIN, CB), F32), pltpu.VMEM((nc, CONV_HIST, CB), F32),
            pltpu.VMEM((nc, SUBLANES, CB), F32),
        ],
        compiler_params=pltpu.CompilerParams(
            dimension_semantics=("arbitrary", "arbitrary", "arbitrary"),
            vmem_limit_bytes=V7X_SCOPED_VMEM_BYTES),
        name="mixer",
    )(x, g1, w_in_b, w_in_b, w_in_b, w_in_b, w_in_b, pool_w_b, ps, cw, cbias,
      gaw_b, gab, gxw_b, gxb, lam, w_out_b, vp_meta, vl_meta, h_meta, mlp_w1[0], mlp_w2[0])

    rows = batch * seq
    out = pl.pallas_call(
        _mlp_kernel,
        grid=(rows // tmb, d_ff // tf),
        in_specs=[
            pl.BlockSpec((tmb, d), lambda m, f: (m, 0)),
            pl.BlockSpec((1, d), lambda m, f: (0, 0)),
            pl.BlockSpec((d, tf), lambda m, f: (0, f)),
            pl.BlockSpec((tf, d), lambda m, f: (f, 0)),
            pl.BlockSpec((1, d), lambda m, f: (0, 0)),
        ],
        out_specs=pl.BlockSpec((tmb, d), lambda m, f: (m, 0)),
        out_shape=jax.ShapeDtypeStruct((rows, d), F32),
        scratch_shapes=[pltpu.VMEM((tmb, d), BF16)],
        compiler_params=pltpu.CompilerParams(
            dimension_semantics=("arbitrary", "arbitrary"),
            vmem_limit_bytes=V7X_SCOPED_VMEM_BYTES),
        name="mlp",
    )(h1.reshape(rows, d), g2, w1_b, w2_b, gf)
    return out.reshape(batch, seq, d)
```

```python
import functools

import jax
import jax.numpy as jnp
from jax import lax
from jax.experimental import pallas as pl
from jax.experimental.pallas import tpu as pltpu

N_META = 16
POOL_WINDOWS = (2, 4, 8, 16)
POOL_GROUP_DIM = 512
LRU_HEAD_DIM = 256
CONV_WIDTH = 4
LRU_C = 8.0
NORM_EPS = 1e-6

SUBLANES = 8
CB = POOL_GROUP_DIM
HEADS_PER_CB = CB // LRU_HEAD_DIM
MAX_WIN = max(POOL_WINDOWS)
CONV_HIST = SUBLANES
V7X_SCOPED_VMEM_BYTES = 60000 * 1024

F32 = jnp.float32
BF16 = jnp.bfloat16

_dot = functools.partial(jnp.dot, preferred_element_type=F32)


def _rmsnorm(x, g):
    return x * lax.rsqrt(jnp.mean(x * x, axis=-1, keepdims=True) + NORM_EPS) * g


def _sigmoid(x):
    return 0.5 * jnp.tanh(0.5 * x) + 0.5


def _sqrt_nonneg(y):
    return jnp.where(y > 0.0, y * lax.rsqrt(y), 0.0)


def _shift_rows(e, k):
    n, w = e.shape
    if k % SUBLANES == 0:
        return jnp.concatenate([e[n - k:], e[:n - k]], axis=0)
    assert k < SUBLANES
    r = pltpu.roll(e.reshape(n // SUBLANES, SUBLANES, w), k, 1)
    prev = jnp.concatenate([r[-1:], r[:-1]], axis=0)
    rows = lax.broadcasted_iota(jnp.int32, r.shape, 1)
    return jnp.where(rows < k, prev, r).reshape(n, w)


def _window_sums(hist, vp):
    s = jnp.concatenate([hist, vp], axis=0)
    sums = []
    shift = 1
    while shift < MAX_WIN:
        s = s + _shift_rows(s, shift)
        sums.append(s)
        shift *= 2
    return sums


def _causal_conv(hist, vl, cw, cbias):
    e = jnp.concatenate([hist, vl], axis=0)
    acc = cbias + cw[CONV_WIDTH - 1:CONV_WIDTH, :] * e
    for k in range(CONV_WIDTH - 1):
        acc = acc + cw[k:k + 1, :] * _shift_rows(e, CONV_WIDTH - 1 - k)
    return acc[CONV_HIST:, :]


def _lru_coeffs(xc, gaw_ref, gab, gxw_ref, gxb, lam):
    xcb = xc.astype(BF16)
    r_parts, i_parts = [], []
    for h in range(HEADS_PER_CB):
        xh = xcb[:, h * LRU_HEAD_DIM:(h + 1) * LRU_HEAD_DIM]
        r_parts.append(_dot(xh, gaw_ref[h]))
        i_parts.append(_dot(xh, gxw_ref[h]))
    r = _sigmoid(jnp.concatenate(r_parts, axis=-1) + gab)
    i = _sigmoid(jnp.concatenate(i_parts, axis=-1) + gxb)
    nlam = -lam
    softplus = jnp.maximum(nlam, 0.0) + jnp.log1p(jnp.exp(-jnp.abs(nlam)))
    log_a = (-LRU_C) * r * softplus
    a = jnp.exp(log_a)
    mult = _sqrt_nonneg(1.0 - a * a)
    return a, mult * (i * xc)


def _linear_scan(a, b, h_prev, a_ref, b_ref, h_ref, n):
    a = a.reshape(n // SUBLANES, SUBLANES, CB)
    b = b.reshape(n // SUBLANES, SUBLANES, CB)
    rows = lax.broadcasted_iota(jnp.int32, a.shape, 1)
    k = 1
    while k < SUBLANES:
        inside = rows >= k
        a_sh = jnp.where(inside, pltpu.roll(a, k, 1), 1.0)
        b_sh = jnp.where(inside, pltpu.roll(b, k, 1), 0.0)
        b = a * b_sh + b
        a = a * a_sh
        k *= 2
    a = a.reshape(n, CB)
    b = b.reshape(n, CB)
    a_ref[...] = a
    b_ref[...] = b
    hp = h_prev
    for g in range(n // SUBLANES):
        sl = pl.ds(g * SUBLANES, SUBLANES)
        hg = a_ref[sl, :] * hp + b_ref[sl, :]
        h_ref[sl, :] = hg
        hp = jnp.broadcast_to(hg[SUBLANES - 1:SUBLANES, :], (SUBLANES, CB))
    return hp


def _meta_kernel(meta_ref, g1_ref, wp_ref, wl_ref, cw_ref, cbias_ref, gaw_ref, gab_ref,
                 gxw_ref, gxb_ref, lam_ref, vp_out, vl_out, h_out,
                 u_scr, a_scr, b_scr, h_scr):
    c = pl.program_id(0)

    @pl.when(c == 0)
    def _():
        u_scr[...] = _rmsnorm(meta_ref[...], g1_ref[...]).astype(BF16)

    u = u_scr[...]
    vp = _dot(u, wp_ref[...])
    vl = _dot(u, wl_ref[...])
    vp_out[...] = vp
    vl_out[...] = vl[N_META - CONV_HIST:, :]
    xc = _causal_conv(jnp.zeros((CONV_HIST, CB), F32), vl, cw_ref[...], cbias_ref[...])
    a, b = _lru_coeffs(xc, gaw_ref, gab_ref[...], gxw_ref, gxb_ref[...], lam_ref[...])
    h_out[...] = _linear_scan(a, b, jnp.zeros((SUBLANES, CB), F32), a_scr, b_scr, h_scr, N_META)


def _mixer_kernel(x_ref, g1_ref, w0_ref, w1_ref, w2_ref, w3_ref, w4_ref, pw_ref, ps_ref,
                  cw_ref, cbias_ref, gaw_ref, gab_ref, gxw_ref, gxb_ref, lam_ref, wo_ref,
                  vpm_ref, vlm_ref, hm_ref, mw1_ref, mw2_ref, out_ref, mw1b_ref, mw2b_ref,
                  u_scr, a_scr, b_scr, h_scr,
                  pool_hist, conv_hist, h_state, *, tm):
    t = pl.program_id(1)
    c = pl.program_id(2)

    mw1b_ref[...] = mw1_ref[...].astype(BF16)
    mw2b_ref[...] = mw2_ref[...].astype(BF16)

    @pl.when(c == 0)
    def _():
        xt = x_ref[0]
        u_scr[...] = _rmsnorm(xt, g1_ref[...]).astype(BF16)
        out_ref[0] = xt

    @pl.when(t == 0)
    def _():
        pool_hist[c] = vpm_ref[...]
        conv_hist[c] = vlm_ref[...]
        h_state[c] = hm_ref[...]

    u = u_scr[...]
    vl = _dot(u, w1_ref[...])
    vp = _dot(u, w0_ref[...])

    xc = _causal_conv(conv_hist[c], vl, cw_ref[...], cbias_ref[...])
    conv_hist[c] = vl[tm - CONV_HIST:, :]
    a, b = _lru_coeffs(xc, gaw_ref, gab_ref[...], gxw_ref, gxb_ref[...], lam_ref[...])
    gelu_gate = jax.nn.gelu(_dot(u, w2_ref[...]))

    sums = _window_sums(pool_hist[c], vp)
    pool_hist[c] = vp[tm - MAX_WIN:, :]
    s = sums[-1]
    inv_w = jnp.float32(1.0 / POOL_WINDOWS[-1])
    for g in range(len(POOL_WINDOWS) - 2, -1, -1):
        s = jnp.where(c == g, sums[g], s)
        inv_w = jnp.where(c == g, jnp.float32(1.0 / POOL_WINDOWS[g]), inv_w)
    d = s[MAX_WIN:, :] * inv_w - vp
    pool_gate = _sigmoid(_dot(u, w3_ref[...]))
    pool_term = pool_gate * (_dot(d.astype(BF16), pw_ref[0]) * ps_ref[...])
    lru_gate = _sigmoid(_dot(u, w4_ref[...])) * gelu_gate

    h_state[c] = _linear_scan(a, b, h_state[c], a_scr, b_scr, h_scr, tm)
    merged = pool_term + lru_gate * h_scr[...]
    out_ref[0] += _dot(merged.astype(BF16), wo_ref[...])


def _mlp_kernel(h_ref, g2_ref, w1_ref, w2_ref, gf_ref, out_ref, u_scr):
    f = pl.program_id(1)

    @pl.when(f == 0)
    def _():
        h = h_ref[...]
        u_scr[...] = _rmsnorm(h, g2_ref[...]).astype(BF16)
        out_ref[...] = h

    a = jnp.square(jnp.maximum(_dot(u_scr[...], w1_ref[...]), 0.0))
    out_ref[...] += _dot(a.astype(BF16), w2_ref[...])

    @pl.when(f == pl.num_programs(1) - 1)
    def _():
        out_ref[...] = _rmsnorm(out_ref[...], gf_ref[...])


def _tiles(seq, d_ff):
    tm = 512 if seq % 512 == 0 else seq
    tf = 2048 if d_ff % 2048 == 0 else d_ff
    return tm, tm, tf


def kernel(x, meta_tokens, norm1_g, w_in, pool_w, pool_scale, conv_w, conv_b, gate_a_w, gate_a_b,
           gate_x_w, gate_x_b, lru_lambda, w_out, norm2_g, mlp_w1, mlp_w2, final_g):
    batch, seq, d = x.shape
    depth = w_in.shape[0]
    d_ff = mlp_w1.shape[-1]
    assert depth == 1 and meta_tokens.shape == (N_META, d)
    assert d % CB == 0 and pool_w.shape[1:] == (d // CB, CB, CB)
    assert w_in.shape == (1, d, 5 * d) and conv_w.shape == (1, CONV_WIDTH, d)
    nc = d // CB
    tm, tmb, tf = _tiles(seq, d_ff)
    assert seq % tm == 0 and tm % SUBLANES == 0 and tm >= MAX_WIN

    row = lambda v: v.reshape(1, -1).astype(F32)
    w_in_b = w_in[0].astype(BF16)
    pool_w_b = pool_w[0].astype(BF16)
    gaw_b = gate_a_w[0].astype(BF16)
    gxw_b = gate_x_w[0].astype(BF16)
    w_out_b = w_out[0].astype(BF16)
    g1, g2, gf = row(norm1_g[0]), row(norm2_g[0]), row(final_g)
    ps, cbias, lam = row(pool_scale[0]), row(conv_b[0]), row(lru_lambda[0])
    gab, gxb = row(gate_a_b[0]), row(gate_x_b[0])
    cw = conv_w[0]

    chan = lambda c: (0, c)
    meta_specs = [
        pl.BlockSpec((N_META, d), lambda c: (0, 0)),
        pl.BlockSpec((1, d), lambda c: (0, 0)),
        pl.BlockSpec((d, CB), lambda c: (0, c)),
        pl.BlockSpec((d, CB), lambda c: (0, nc + c)),
        pl.BlockSpec((CONV_WIDTH, CB), chan),
        pl.BlockSpec((1, CB), chan),
        pl.BlockSpec((HEADS_PER_CB, LRU_HEAD_DIM, LRU_HEAD_DIM), lambda c: (c, 0, 0)),
        pl.BlockSpec((1, CB), chan),
        pl.BlockSpec((HEADS_PER_CB, LRU_HEAD_DIM, LRU_HEAD_DIM), lambda c: (c, 0, 0)),
        pl.BlockSpec((1, CB), chan),
        pl.BlockSpec((1, CB), chan),
    ]
    vp_meta, vl_meta, h_meta = pl.pallas_call(
        _meta_kernel,
        grid=(nc,),
        in_specs=meta_specs,
        out_specs=[pl.BlockSpec((N_META, CB), chan), pl.BlockSpec((CONV_HIST, CB), chan),
                   pl.BlockSpec((SUBLANES, CB), chan)],
        out_shape=[jax.ShapeDtypeStruct((N_META, d), F32), jax.ShapeDtypeStruct((CONV_HIST, d), F32),
                   jax.ShapeDtypeStruct((SUBLANES, d), F32)],
        scratch_shapes=[pltpu.VMEM((N_META, d), BF16),
                        pltpu.VMEM((N_META, CB), F32), pltpu.VMEM((N_META, CB), F32),
                        pltpu.VMEM((N_META, CB), F32)],
        compiler_params=pltpu.CompilerParams(dimension_semantics=("arbitrary",)),
        name="meta_prologue",
    )(meta_tokens, g1, w_in_b, w_in_b, cw, cbias, gaw_b, gab, gxw_b, gxb, lam)

    nt = seq // tm
    nsteps = batch * nt * nc
    bf16_rows = 2 * SUBLANES
    assert d % (nsteps * bf16_rows) == 0 and d_ff % (nsteps * bf16_rows) == 0
    step_slab = lambda b, t, c: ((b * nt + t) * nc + c, 0)
    chan3 = lambda b, t, c: (0, c)
    w_in_spec = lambda j: pl.BlockSpec((d, CB), lambda b, t, c, j=j: (0, j * nc + c))
    head_spec = pl.BlockSpec((HEADS_PER_CB, LRU_HEAD_DIM, LRU_HEAD_DIM), lambda b, t, c: (c, 0, 0))
    x_spec = pl.BlockSpec((1, tm, d), lambda b, t, c: (b, t, 0))
    mixer_specs = [
        x_spec,
        pl.BlockSpec((1, d), lambda b, t, c: (0, 0)),
        w_in_spec(0), w_in_spec(1), w_in_spec(2), w_in_spec(3), w_in_spec(4),
        pl.BlockSpec((1, CB, CB), lambda b, t, c: (c, 0, 0)),
        pl.BlockSpec((1, CB), chan3),
        pl.BlockSpec((CONV_WIDTH, CB), chan3),
        pl.BlockSpec((1, CB), chan3),
        head_spec, pl.BlockSpec((1, CB), chan3),
        head_spec, pl.BlockSpec((1, CB), chan3),
        pl.BlockSpec((1, CB), chan3),
        pl.BlockSpec((CB, d), lambda b, t, c: (c, 0)),
        pl.BlockSpec((N_META, CB), chan3),
        pl.BlockSpec((CONV_HIST, CB), chan3),
        pl.BlockSpec((SUBLANES, CB), chan3),
        pl.BlockSpec((d // nsteps, d_ff), step_slab),
        pl.BlockSpec((d_ff // nsteps, d), step_slab),
    ]
    h1, w1_b, w2_b = pl.pallas_call(
        functools.partial(_mixer_kernel, tm=tm),
        grid=(batch, nt, nc),
        in_specs=mixer_specs,
        out_specs=[x_spec, pl.BlockSpec((d // nsteps, d_ff), step_slab),
                   pl.BlockSpec((d_ff // nsteps, d), step_slab)],
        out_shape=[jax.ShapeDtypeStruct((batch, seq, d), F32),
                   jax.ShapeDtypeStruct((d, d_ff), BF16), jax.ShapeDtypeStruct((d_ff, d), BF16)],
        scratch_shapes=[
            pltpu.VMEM((tm, d), BF16),
            pltpu.VMEM((tm, CB), F32), pltpu.VMEM((tm, CB), F32), pltpu.VMEM((tm, CB), F32),
            pltpu.VMEM((nc, MAX_WIN, CB), F32), pltpu.VMEM((nc, CONV_HIST, CB), F32),
            pltpu.VMEM((nc, SUBLANES, CB), F32),
        ],
        compiler_params=pltpu.CompilerParams(
            dimension_semantics=("arbitrary", "arbitrary", "arbitrary"),
            vmem_limit_bytes=V7X_SCOPED_VMEM_BYTES),
        name="mixer",
    )(x, g1, w_in_b, w_in_b, w_in_b, w_in_b, w_in_b, pool_w_b, ps, cw, cbias,
      gaw_b, gab, gxw_b, gxb, lam, w_out_b, vp_meta, vl_meta, h_meta, mlp_w1[0], mlp_w2[0])

    rows = batch * seq
    out = pl.pallas_call(
        _mlp_kernel,
        grid=(rows // tmb, d_ff // tf),
        in_specs=[
            pl.BlockSpec((tmb, d), lambda m, f: (m, 0)),
            pl.BlockSpec((1, d), lambda m, f: (0, 0)),
            pl.BlockSpec((d, tf), lambda m, f: (0, f)),
            pl.BlockSpec((tf, d), lambda m, f: (f, 0)),
            pl.BlockSpec((1, d), lambda m, f: (0, 0)),
        ],
        out_specs=pl.BlockSpec((tmb, d), lambda m, f: (m, 0)),
        out_shape=jax.ShapeDtypeStruct((rows, d), F32),
        scratch_shapes=[pltpu.VMEM((tmb, d), BF16)],
        compiler_params=pltpu.CompilerParams(
            dimension_semantics=("arbitrary", "arbitrary"),
            vmem_limit_bytes=V7X_SCOPED_VMEM_BYTES),
        name="mlp",
    )(h1.reshape(rows, d), g2, w1_b, w2_b, gf)
    return out.reshape(batch, seq, d)
```
